```python
import jax, jax.numpy as jnp
from jax import lax
import numpy as np

D_MODEL = 2048
BATCH = 2
SEQ = 4096
DEPTH = 4

EXPAND = 2
D_INNER = EXPAND * D_MODEL
PLE_DIM = 256
N_MIXERS = 2
POOL_WINDOWS = (2, 4, 8, 16)
N_POOL_GROUPS = len(POOL_WINDOWS)
POOL_GROUP_DIM = D_INNER // N_POOL_GROUPS
LRU_HEADS = 16
LRU_BLOCK = D_INNER // LRU_HEADS
CONV_WIDTH = 4
LRU_C = 8.0
RMS_EPS = 1e-6
N_POOL_LAYERS = (DEPTH + 1) // 2
N_LRU_LAYERS = DEPTH // 2

kernel_name = "hybrid_pool_rglru_sandwich_ple"


def rmsnorm(x, g):
    xf = x.astype(jnp.float32)
    var = jnp.mean(xf * xf, axis=-1, keepdims=True)
    return (xf * lax.rsqrt(var + RMS_EPS) * g.astype(jnp.float32)).astype(x.dtype)


def pool_mixer(u, w_grp, b, scale):
    B, S, E = u.shape
    maxw = POOL_WINDOWS[-1]
    uf = u.astype(jnp.float32)
    cs = jnp.cumsum(uf, axis=1)
    csp = jnp.pad(cs, ((0, 0), (maxw, 0), (0, 0)))
    pos = jnp.arange(1, S + 1, dtype=jnp.int32)
    diffs = []
    for g, w in enumerate(POOL_WINDOWS):
        c0, c1 = g * POOL_GROUP_DIM, (g + 1) * POOL_GROUP_DIM
        win_sum = csp[:, maxw:, c0:c1] - csp[:, maxw - w:maxw - w + S, c0:c1]
        count = jnp.minimum(pos, w).astype(jnp.float32)[None, :, None]
        diffs.append(win_sum / count - uf[..., c0:c1])
    d = jnp.stack(diffs, axis=2).astype(u.dtype)
    y = jnp.einsum('bsgc,gcd->bsgd', d, w_grp).reshape(B, S, E) + b
    return y * scale


def causal_depthwise_conv(x, w, b):
    S = x.shape[1]
    xp = jnp.pad(x, ((0, 0), (CONV_WIDTH - 1, 0), (0, 0)))
    y = b
    for k in range(CONV_WIDTH):
        y = y + xp[:, k:k + S] * w[k]
    return y


def rglru(x, wa, ba, wx, bx, lam):
    B, S, E = x.shape
    xb = x.reshape(B, S, LRU_HEADS, LRU_BLOCK)
    r = jax.nn.sigmoid((jnp.einsum('bshi,hij->bshj', xb, wa).reshape(B, S, E) + ba).astype(jnp.float32))
    ig = jax.nn.sigmoid((jnp.einsum('bshi,hij->bshj', xb, wx).reshape(B, S, E) + bx).astype(jnp.float32))
    log_a = -LRU_C * r * jax.nn.softplus(-lam.astype(jnp.float32))
    a = jnp.exp(log_a)
    mult = jnp.sqrt(jnp.maximum(-jnp.expm1(2.0 * log_a), 0.0))
    bterm = mult * ig * x.astype(jnp.float32)

    def combine(left, right):
        a1, b1 = left
        a2, b2 = right
        return a1 * a2, a2 * b1 + b2

    _, h = lax.associative_scan(combine, (a, bterm), axis=1)
    return h.astype(x.dtype)


def setup_inputs(seed: int = 0) -> dict:
    key = jax.random.key(seed)
    ks = jax.random.split(key, 24)
    f32 = jnp.float32

    def nrm(k, shape, fan_in):
        return jax.random.normal(k, shape, f32) * (fan_in ** -0.5)

    def gain(k, shape):
        return 1.0 + 0.05 * jax.random.normal(k, shape, f32)

    def bias(k, shape):
        return 0.01 * jax.random.normal(k, shape, f32)

    a0 = jax.random.uniform(ks[15], (N_LRU_LAYERS, D_INNER), f32, 0.9, 0.999)
    lru_L = jnp.log(a0) - jnp.log1p(-a0)
    return {
        "x": jax.random.normal(ks[0], (BATCH, SEQ, D_MODEL), f32),
        "p": jax.random.normal(ks[1], (DEPTH, BATCH, SEQ, PLE_DIM), f32),
        "w_in": nrm(ks[2], (DEPTH, D_MODEL, 2 * D_INNER), D_MODEL),
        "w_out": nrm(ks[3], (DEPTH, D_INNER, D_MODEL), D_INNER),
        "g_pre": gain(ks[4], (DEPTH, D_MODEL)),
        "g_post": gain(ks[5], (DEPTH, D_MODEL)),
        "pool_w": nrm(ks[6], (N_POOL_LAYERS, N_POOL_GROUPS, POOL_GROUP_DIM, POOL_GROUP_DIM), POOL_GROUP_DIM),
        "pool_b": bias(ks[7], (N_POOL_LAYERS, D_INNER)),
        "pool_scale": gain(ks[8], (N_POOL_LAYERS, D_INNER)),
        "conv_w": nrm(ks[9], (N_LRU_LAYERS, CONV_WIDTH, D_INNER), CONV_WIDTH),
        "conv_b": bias(ks[10], (N_LRU_LAYERS, D_INNER)),
        "lru_wa": nrm(ks[11], (N_LRU_LAYERS, LRU_HEADS, LRU_BLOCK, LRU_BLOCK), LRU_BLOCK),
        "lru_ba": bias(ks[12], (N_LRU_LAYERS, D_INNER)),
        "lru_wx": nrm(ks[13], (N_LRU_LAYERS, LRU_HEADS, LRU_BLOCK, LRU_BLOCK), LRU_BLOCK),
        "lru_bx": bias(ks[14], (N_LRU_LAYERS, D_INNER)),
        "lru_L": lru_L,
        "w_ple": nrm(ks[16], (DEPTH, PLE_DIM, D_MODEL), PLE_DIM),
        "w_ple_gate": nrm(ks[17], (DEPTH, D_MODEL, D_MODEL), D_MODEL),
        "g_ple_in": gain(ks[18], (DEPTH, D_MODEL)),
        "g_ple_out": gain(ks[19], (DEPTH, D_MODEL)),
    }


def reference(x, p, w_in, w_out, g_pre, g_post, pool_w, pool_b, pool_scale,
              conv_w, conv_b, lru_wa, lru_ba, lru_wx, lru_bx, lru_L,
              w_ple, w_ple_gate, g_ple_in, g_ple_out):
    for i in range(DEPTH):
        h = rmsnorm(x, g_pre[i])
        uz = jnp.einsum('bsd,de->bse', h, w_in[i])
        u, z = uz[..., :D_INNER], uz[..., D_INNER:]
        j = i // N_MIXERS
        if i % N_MIXERS == 0:
            y = pool_mixer(u, pool_w[j], pool_b[j], pool_scale[j])
        else:
            uc = causal_depthwise_conv(u, conv_w[j], conv_b[j])
            y = rglru(uc, lru_wa[j], lru_ba[j], lru_wx[j], lru_bx[j], lru_L[j])
        y = y * jax.nn.silu(z)
        o = jnp.einsum('bse,ed->bsd', y, w_out[i])
        x = x + rmsnorm(o, g_post[i])
        gate = jax.nn.sigmoid(jnp.einsum('bsd,de->bse', rmsnorm(x, g_ple_in[i]), w_ple_gate[i]))
        e = jnp.einsum('bsk,kd->bsd', p[i], w_ple[i])
        x = x + rmsnorm(e * gate, g_ple_out[i])
    return x
```

```python
import functools

import jax
import jax.numpy as jnp
from jax import lax
from jax.experimental import pallas as pl
from jax.experimental.pallas import tpu as pltpu

F32 = jnp.float32
BF16 = jnp.bfloat16

POOL_WINDOWS = (2, 4, 8, 16)
LRU_BLOCK = 256
CONV_WIDTH = 4
LRU_C = 8.0
RMS_EPS = 1e-6

SUBLANES = 8
COL_BLOCK = 1024
POOL_HIST = 16
CONV_HIST = SUBLANES
MIX_ROWS = 512
OUT_ROWS = 256
NORM_ROWS = 512
VMEM_LIMIT = 56 * 1024 * 1024


def _rms(xf, g):
    var = jnp.mean(xf * xf, axis=-1, keepdims=True)
    return xf * lax.rsqrt(var + RMS_EPS) * g


def _silu(z):
    return z * jax.nn.sigmoid(z)


def _prenorm_kernel(x_ref, g_ref, h_ref):
    h_ref[...] = _rms(x_ref[...], g_ref[...]).astype(BF16)


def _prenorm(x2, g):
    t, d = x2.shape
    return pl.pallas_call(
        _prenorm_kernel,
        grid=(t // NORM_ROWS,),
        in_specs=[pl.BlockSpec((NORM_ROWS, d), lambda i: (i, 0)),
                  pl.BlockSpec((1, d), lambda i: (0, 0))],
        out_specs=pl.BlockSpec((NORM_ROWS, d), lambda i: (i, 0)),
        out_shape=jax.ShapeDtypeStruct((t, d), BF16),
        compiler_params=pltpu.CompilerParams(dimension_semantics=("arbitrary",)),
        name="prenorm",
    )(x2, g)


def _pool_kernel(h_ref, wu_ref, wz_ref, pw_ref, pb_ref, ps_ref, y_ref, ubuf, dbuf):
    g = pl.program_id(0)
    s = pl.program_id(2)
    tm = h_ref.shape[1]
    h = h_ref[0]
    u = jnp.dot(h, wu_ref[...], preferred_element_type=F32)
    z = jnp.dot(h, wz_ref[...], preferred_element_type=F32)

    @pl.when(s == 0)
    def _():
        ubuf[0:POOL_HIST, :] = jnp.zeros((POOL_HIST, ubuf.shape[1]), F32)

    @pl.when(s > 0)
    def _():
        ubuf[0:POOL_HIST, :] = ubuf[tm:tm + POOL_HIST, :]

    ubuf[POOL_HIST:, :] = u

    pos = s * tm + lax.broadcasted_iota(jnp.int32, (tm, 1), 0) + 1
    for gi, w in enumerate(POOL_WINDOWS):
        @pl.when(g == gi)
        def _(w=w):
            acc = ubuf[...]
            k = 1
            while k < w:
                acc = acc + pltpu.roll(acc, k, axis=0)
                k *= 2
            inv = 1.0 / jnp.minimum(pos, w).astype(F32)
            dbuf[...] = (acc[POOL_HIST:, :] * inv - ubuf[POOL_HIST:, :]).astype(BF16)

    yp = jnp.dot(dbuf[...], pw_ref[0], preferred_element_type=F32)
    yp = (yp + pb_ref[...]) * ps_ref[...]
    y_ref[0] = (yp * _silu(z)).astype(BF16)


def _pool_layer(h3, w_in, pool_w, pool_b, pool_scale):
    b, s, d = h3.shape
    e = pool_b.shape[-1]
    ncb = e // COL_BLOCK
    tm = MIX_ROWS
    return pl.pallas_call(
        _pool_kernel,
        grid=(ncb, b, s // tm),
        in_specs=[
            pl.BlockSpec((1, tm, d), lambda g, bi, si: (bi, si, 0)),
            pl.BlockSpec((d, COL_BLOCK), lambda g, bi, si: (0, g)),
            pl.BlockSpec((d, COL_BLOCK), lambda g, bi, si: (0, ncb + g)),
            pl.BlockSpec((1, COL_BLOCK, COL_BLOCK), lambda g, bi, si: (g, 0, 0)),
            pl.BlockSpec((1, COL_BLOCK), lambda g, bi, si: (0, g)),
            pl.BlockSpec((1, COL_BLOCK), lambda g, bi, si: (0, g)),
        ],
        out_specs=pl.BlockSpec((1, tm, COL_BLOCK), lambda g, bi, si: (bi, si, g)),
        out_shape=jax.ShapeDtypeStruct((b, s, e), BF16),
        scratch_shapes=[pltpu.VMEM((POOL_HIST + tm, COL_BLOCK), F32),
                        pltpu.VMEM((tm, COL_BLOCK), BF16)],
        compiler_params=pltpu.CompilerParams(
            dimension_semantics=("arbitrary", "arbitrary", "arbitrary"),
            vmem_limit_bytes=VMEM_LIMIT),
        name="pool_mixer",
    )(h3, w_in, w_in, pool_w, pool_b, pool_scale)


def _lru_kernel(h_ref, wu_ref, wz_ref, cw_ref, cb_ref, wa_ref, ba_ref, wx_ref, bx_ref, lam_ref,
                y_ref, ubuf, abuf, bbuf, zbuf, carry):
    s = pl.program_id(2)
    tm = h_ref.shape[1]
    nc = ubuf.shape[1]
    h = h_ref[0]
    u = jnp.dot(h, wu_ref[...], preferred_element_type=F32)
    z = jnp.dot(h, wz_ref[...], preferred_element_type=F32)
    zbuf[...] = _silu(z)

    @pl.when(s == 0)
    def _():
        ubuf[0:CONV_HIST, :] = jnp.zeros((CONV_HIST, nc), F32)
        carry[...] = jnp.zeros(carry.shape, F32)

    @pl.when(s > 0)
    def _():
        ubuf[0:CONV_HIST, :] = ubuf[tm:tm + CONV_HIST, :]

    ubuf[CONV_HIST:, :] = u

    ext = ubuf[...]
    uc = cb_ref[...] + ext * cw_ref[CONV_WIDTH - 1:CONV_WIDTH, :]
    for k in range(CONV_WIDTH - 1):
        shift = CONV_WIDTH - 1 - k
        uc = uc + pltpu.roll(ext, shift, axis=0) * cw_ref[k:k + 1, :]
    uc = uc[CONV_HIST:, :]
    ucb = uc.astype(BF16)

    neg_c_sp = -LRU_C * jax.nn.softplus(-lam_ref[...])
    for hd in range(nc // LRU_BLOCK):
        cs = slice(hd * LRU_BLOCK, (hd + 1) * LRU_BLOCK)
        xb = ucb[:, cs]
        r = jax.nn.sigmoid(jnp.dot(xb, wa_ref[hd], preferred_element_type=F32) + ba_ref[:, cs])
        ig = jax.nn.sigmoid(jnp.dot(xb, wx_ref[hd], preferred_element_type=F32) + bx_ref[:, cs])
        log_a = r * neg_c_sp[:, cs]
        a = jnp.exp(log_a)
        abuf[:, cs] = a
        mult = jnp.sqrt(jnp.maximum(-jnp.tanh(log_a) * (a * a + 1.0), 0.0))
        bbuf[:, cs] = mult * ig * uc[:, cs]

    row = lax.broadcasted_iota(jnp.int32, (SUBLANES, nc), 0)

    def body(i, hprev):
        r0 = pl.multiple_of(i * SUBLANES, SUBLANES)
        a = abuf[pl.ds(r0, SUBLANES), :]
        bb = bbuf[pl.ds(r0, SUBLANES), :]
        k = 1
        while k < SUBLANES:
            keep = row >= k
            a_sh = jnp.where(keep, pltpu.roll(a, k, axis=0), 1.0)
            b_sh = jnp.where(keep, pltpu.roll(bb, k, axis=0), 0.0)
            bb = a * b_sh + bb
            a = a * a_sh
            k *= 2
        hh = a * hprev + bb
        y_ref[0, pl.ds(r0, SUBLANES), :] = (hh * zbuf[pl.ds(r0, SUBLANES), :]).astype(BF16)
        return jnp.broadcast_to(hh[SUBLANES - 1:SUBLANES, :], (SUBLANES, nc))

    carry[...] = lax.fori_loop(0, tm // SUBLANES, body, carry[...])


def _lru_layer(h3, w_in, conv_w, conv_b, wa, ba, wx, bx, lam):
    b, s, d = h3.shape
    e = conv_b.shape[-1]
    ncb = e // COL_BLOCK
    hpb = COL_BLOCK // LRU_BLOCK
    tm = MIX_ROWS
    vec = pl.BlockSpec((1, COL_BLOCK), lambda c, bi, si: (0, c))
    return pl.pallas_call(
        _lru_kernel,
        grid=(ncb, b, s // tm),
        in_specs=[
            pl.BlockSpec((1, tm, d), lambda c, bi, si: (bi, si, 0)),
            pl.BlockSpec((d, COL_BLOCK), lambda c, bi, si: (0, c)),
            pl.BlockSpec((d, COL_BLOCK), lambda c, bi, si: (0, ncb + c)),
            pl.BlockSpec((CONV_WIDTH, COL_BLOCK), lambda c, bi, si: (0, c)),
            vec,
            pl.BlockSpec((hpb, LRU_BLOCK, LRU_BLOCK), lambda c, bi, si: (c, 0, 0)),
            vec,
            pl.BlockSpec((hpb, LRU_BLOCK, LRU_BLOCK), lambda c, bi, si: (c, 0, 0)),
            vec,
            vec,
        ],
        out_specs=pl.BlockSpec((1, tm, COL_BLOCK), lambda c, bi, si: (bi, si, c)),
        out_shape=jax.ShapeDtypeStruct((b, s, e), BF16),
        scratch_shapes=[pltpu.VMEM((CONV_HIST + tm, COL_BLOCK), F32),
                        pltpu.VMEM((tm, COL_BLOCK), F32),
                        pltpu.VMEM((tm, COL_BLOCK), F32),
                        pltpu.VMEM((tm, COL_BLOCK), F32),
                        pltpu.VMEM((SUBLANES, COL_BLOCK), F32)],
        compiler_params=pltpu.CompilerParams(
            dimension_semantics=("arbitrary", "arbitrary", "arbitrary"),
            vmem_limit_bytes=VMEM_LIMIT),
        name="lru_mixer",
    )(h3, w_in, w_in, conv_w, conv_b, wa, ba, wx, bx, lam)


def _out_kernel(y_ref, x_ref, p_ref, wo_ref, wg_ref, wp_ref, gpost_ref, gin_ref, gout_ref, gnext_ref,
                xo_ref, hn_ref):
    o = jnp.dot(y_ref[...], wo_ref[...], preferred_element_type=F32)
    x1 = x_ref[...] + _rms(o, gpost_ref[...])
    xn = _rms(x1, gin_ref[...]).astype(BF16)
    gate = jax.nn.sigmoid(jnp.dot(xn, wg_ref[...], preferred_element_type=F32))
    emb = jnp.dot(p_ref[...].astype(BF16), wp_ref[...], preferred_element_type=F32)
    x2 = x1 + _rms(emb * gate, gout_ref[...])
    xo_ref[...] = x2
    hn_ref[...] = _rms(x2, gnext_ref[...]).astype(BF16)


def _out_layer(y2, x2, p2, w_out, w_gate, w_ple, g_post, g_in, g_out, g_next):
    t, e = y2.shape
    d = x2.shape[1]
    k = p2.shape[1]
    tm = OUT_ROWS
    resident = pl.Buffered(1)
    vec = pl.BlockSpec((1, d), lambda i: (0, 0))
    return pl.pallas_call(
        _out_kernel,
        grid=(t // tm,),
        in_specs=[
            pl.BlockSpec((tm, e), lambda i: (i, 0)),
            pl.BlockSpec((tm, d), lambda i: (i, 0)),
            pl.BlockSpec((tm, k), lambda i: (i, 0)),
            pl.BlockSpec((e, d), lambda i: (0, 0), pipeline_mode=resident),
            pl.BlockSpec((d, d), lambda i: (0, 0), pipeline_mode=resident),
            pl.BlockSpec((k, d), lambda i: (0, 0), pipeline_mode=resident),
            vec, vec, vec, vec,
        ],
        out_specs=[pl.BlockSpec((tm, d), lambda i: (i, 0)),
                   pl.BlockSpec((tm, d), lambda i: (i, 0))],
        out_shape=[jax.ShapeDtypeStruct((t, d), F32),
                   jax.ShapeDtypeStruct((t, d), BF16)],
        compiler_params=pltpu.CompilerParams(
            dimension_semantics=("arbitrary",),
            vmem_limit_bytes=VMEM_LIMIT),
        name="out_ple",
    )(y2, x2, p2, w_out, w_gate, w_ple, g_post, g_in, g_out, g_next)


def kernel(x, p, w_in, w_out, g_pre, g_post, pool_w, pool_b, pool_scale, conv_w, conv_b,
           lru_wa, lru_ba, lru_wx, lru_bx, lru_L, w_ple, w_ple_gate, g_ple_in, g_ple_out):
    b, s, d = x.shape
    depth = w_in.shape[0]
    e = w_out.shape[1]
    t = b * s
    row = lambda v: v.reshape(1, -1)

    x2 = x.reshape(t, d)
    h2 = _prenorm(x2, row(g_pre[0]))
    for i in range(depth):
        j = i // 2
        h3 = h2.reshape(b, s, d)
        w_in_i = w_in[i].astype(BF16)
        if i % 2 == 0:
            y3 = _pool_layer(h3, w_in_i, pool_w[j].astype(BF16), row(pool_b[j]), row(pool_scale[j]))
        else:
            y3 = _lru_layer(h3, w_in_i, conv_w[j], row(conv_b[j]),
                            lru_wa[j].astype(BF16), row(lru_ba[j]),
                            lru_wx[j].astype(BF16), row(lru_bx[j]), row(lru_L[j]))
        g_next = g_pre[i + 1] if i + 1 < depth else g_pre[i]
        x2, h2 = _out_layer(y3.reshape(t, e), x2, p[i].reshape(t, -1),
                            w_out[i].astype(BF16), w_ple_gate[i].astype(BF16), w_ple[i].astype(BF16),
                            row(g_post[i]), row(g_ple_in[i]), row(g_ple_out[i]), row(g_next))
    return x2.reshape(b, s, d)
```

```python
import jax
import jax.numpy as jnp
from jax import lax
from jax.experimental import pallas as pl
from jax.experimental.pallas import tpu as pltpu

F32 = jnp.float32
BF16 = jnp.bfloat16

POOL_WINDOWS = (2, 4, 8, 16)
LRU_BLOCK = 256
CONV_WIDTH = 4
LRU_C = 8.0
RMS_EPS = 1e-6

SUBLANES = 8
TILE = 512
SEG = TILE // SUBLANES
COL_BLOCK = 1024
POOL_HIST = (max(POOL_WINDOWS) - 1) * SUBLANES
CONV_HIST = (CONV_WIDTH - 1) * SUBLANES
OUT_ROWS = 256
NORM_ROWS = 512
VMEM_LIMIT = 56 * 1024 * 1024


def _rms(xf, g):
    var = jnp.mean(xf * xf, axis=-1, keepdims=True)
    return xf * lax.rsqrt(var + RMS_EPS) * g


def _silu(z):
    return z * jax.nn.sigmoid(z)


def _to_tile_order(a):
    b, s, c = a.shape
    return a.reshape(b, s // TILE, SUBLANES, SEG, c).swapaxes(2, 3).reshape(b, s, c)


def _from_tile_order(a):
    b, s, c = a.shape
    return a.reshape(b, s // TILE, SEG, SUBLANES, c).swapaxes(2, 3).reshape(b, s, c)


def _history_rows(tail, prev):
    n, c = tail.shape
    seg = lax.broadcasted_iota(jnp.int32, (n, c), 0) % SUBLANES
    src = jnp.where(seg == SUBLANES - 1, prev, tail)
    return pltpu.roll(src.reshape(n // SUBLANES, SUBLANES, c), 1, axis=1).reshape(n, c)


def _prenorm_kernel(x_ref, g_ref, h_ref):
    h_ref[...] = _rms(x_ref[...], g_ref[...]).astype(BF16)


def _prenorm(x2, g):
    t, d = x2.shape
    return pl.pallas_call(
        _prenorm_kernel,
        grid=(t // NORM_ROWS,),
        in_specs=[pl.BlockSpec((NORM_ROWS, d), lambda i: (i, 0)),
                  pl.BlockSpec((1, d), lambda i: (0, 0))],
        out_specs=pl.BlockSpec((NORM_ROWS, d), lambda i: (i, 0)),
        out_shape=jax.ShapeDtypeStruct((t, d), BF16),
        compiler_params=pltpu.CompilerParams(dimension_semantics=("arbitrary",)),
        name="prenorm",
    )(x2, g)


def _pool_kernel(h_ref, wu_ref, wz_ref, pw_ref, pb_ref, ps_ref, y_ref, prev, ubuf, dbuf):
    g = pl.program_id(0)
    si = pl.program_id(2)
    tm = h_ref.shape[1]
    h = h_ref[0]
    u = jnp.dot(h, wu_ref[...], preferred_element_type=F32)
    z = jnp.dot(h, wz_ref[...], preferred_element_type=F32)

    @pl.when(si == 0)
    def _():
        prev[...] = jnp.zeros(prev.shape, F32)

    tail = u[tm - POOL_HIST:, :]
    ubuf[0:POOL_HIST, :] = _history_rows(tail, prev[...])
    ubuf[POOL_HIST:, :] = u
    prev[...] = tail

    row = lax.broadcasted_iota(jnp.int32, (tm, 1), 0)
    pos = si * tm + (row % SUBLANES) * SEG + row // SUBLANES + 1
    for gi, w in enumerate(POOL_WINDOWS):
        @pl.when(g == gi)
        def _(w=w):
            acc = ubuf[POOL_HIST - (w - 1) * SUBLANES:, :]
            k = 1
            while k < w:
                sh = k * SUBLANES
                acc = acc[sh:, :] + acc[:acc.shape[0] - sh, :]
                k *= 2
            inv = 1.0 / jnp.minimum(pos, w).astype(F32)
            dbuf[...] = (acc * inv - ubuf[POOL_HIST:, :]).astype(BF16)

    yp = jnp.dot(dbuf[...], pw_ref[0], preferred_element_type=F32)
    yp = (yp + pb_ref[...]) * ps_ref[...]
    y_ref[0] = (yp * _silu(z)).astype(BF16)


def _pool_layer(h3, w_in, pool_w, pool_b, pool_scale):
    b, s, d = h3.shape
    e = pool_b.shape[-1]
    ncb = e // COL_BLOCK
    tm = TILE
    return pl.pallas_call(
        _pool_kernel,
        grid=(ncb, b, s // tm),
        in_specs=[
            pl.BlockSpec((1, tm, d), lambda g, bi, si: (bi, si, 0)),
            pl.BlockSpec((d, COL_BLOCK), lambda g, bi, si: (0, g)),
            pl.BlockSpec((d, COL_BLOCK), lambda g, bi, si: (0, ncb + g)),
            pl.BlockSpec((1, COL_BLOCK, COL_BLOCK), lambda g, bi, si: (g, 0, 0)),
            pl.BlockSpec((1, COL_BLOCK), lambda g, bi, si: (0, g)),
            pl.BlockSpec((1, COL_BLOCK), lambda g, bi, si: (0, g)),
        ],
        out_specs=pl.BlockSpec((1, tm, COL_BLOCK), lambda g, bi, si: (bi, si, g)),
        out_shape=jax.ShapeDtypeStruct((b, s, e), BF16),
        scratch_shapes=[pltpu.VMEM((POOL_HIST, COL_BLOCK), F32),
                        pltpu.VMEM((POOL_HIST + tm, COL_BLOCK), F32),
                        pltpu.VMEM((tm, COL_BLOCK), BF16)],
        compiler_params=pltpu.CompilerParams(
            dimension_semantics=("arbitrary", "arbitrary", "arbitrary"),
            vmem_limit_bytes=VMEM_LIMIT),
        name="pool_mixer",
    )(h3, w_in, w_in, pool_w, pool_b, pool_scale)


def _lru_kernel(h_ref, wu_ref, wz_ref, cw_ref, cb_ref, wa_ref, ba_ref, wx_ref, bx_ref, lam_ref,
                y_ref, prev, ubuf, abuf, bbuf, zbuf, carry):
    si = pl.program_id(2)
    tm = h_ref.shape[1]
    nc = ubuf.shape[1]
    h = h_ref[0]
    u = jnp.dot(h, wu_ref[...], preferred_element_type=F32)
    z = jnp.dot(h, wz_ref[...], preferred_element_type=F32)
    zbuf[...] = _silu(z)

    @pl.when(si == 0)
    def _():
        prev[...] = jnp.zeros(prev.shape, F32)
        carry[...] = jnp.zeros(carry.shape, F32)

    tail = u[tm - CONV_HIST:, :]
    ubuf[0:CONV_HIST, :] = _history_rows(tail, prev[...])
    ubuf[CONV_HIST:, :] = u
    prev[...] = tail

    uc = cb_ref[...] + ubuf[CONV_HIST:, :] * cw_ref[CONV_WIDTH - 1:CONV_WIDTH, :]
    for k in range(CONV_WIDTH - 1):
        off = CONV_HIST - (CONV_WIDTH - 1 - k) * SUBLANES
        uc = uc + ubuf[off:off + tm, :] * cw_ref[k:k + 1, :]
    ucb = uc.astype(BF16)

    neg_c_sp = -LRU_C * jax.nn.softplus(-lam_ref[...])
    for hd in range(nc // LRU_BLOCK):
        cs = slice(hd * LRU_BLOCK, (hd + 1) * LRU_BLOCK)
        xb = ucb[:, cs]
        r = jax.nn.sigmoid(jnp.dot(xb, wa_ref[hd], preferred_element_type=F32) + ba_ref[:, cs])
        ig = jax.nn.sigmoid(jnp.dot(xb, wx_ref[hd], preferred_element_type=F32) + bx_ref[:, cs])
        log_a = r * neg_c_sp[:, cs]
        a = jnp.exp(log_a)
        abuf[:, cs] = a
        mult = jnp.sqrt(jnp.maximum(-jnp.tanh(log_a) * (a * a + 1.0), 0.0))
        bbuf[:, cs] = mult * ig * uc[:, cs]

    hc = bbuf[0:SUBLANES, :]
    pc = abuf[0:SUBLANES, :]
    for l in range(1, SEG):
        rows = slice(l * SUBLANES, (l + 1) * SUBLANES)
        a = abuf[rows, :]
        hc = a * hc + bbuf[rows, :]
        pc = pc * a
        bbuf[rows, :] = hc
        abuf[rows, :] = pc

    seg = lax.broadcasted_iota(jnp.int32, (SUBLANES, nc), 0)
    k = 1
    while k < SUBLANES:
        keep = seg >= k
        p_sh = jnp.where(keep, pltpu.roll(pc, k, axis=0), 1.0)
        h_sh = jnp.where(keep, pltpu.roll(hc, k, axis=0), 0.0)
        hc = pc * h_sh + hc
        pc = pc * p_sh
        k *= 2
    h_in = jnp.broadcast_to(carry[SUBLANES - 1:SUBLANES, :], (SUBLANES, nc))
    ends = pc * h_in + hc
    carry[...] = ends
    seg_in = jnp.where(seg >= 1, pltpu.roll(ends, 1, axis=0), h_in)

    seg_in2 = jnp.concatenate([seg_in, seg_in], axis=0)
    for l in range(0, SEG, 2):
        rows = slice(l * SUBLANES, (l + 2) * SUBLANES)
        hh = bbuf[rows, :] + abuf[rows, :] * seg_in2
        y_ref[0, rows, :] = (hh * zbuf[rows, :]).astype(BF16)


def _lru_layer(h3, w_in, conv_w, conv_b, wa, ba, wx, bx, lam):
    b, s, d = h3.shape
    e = conv_b.shape[-1]
    ncb = e // COL_BLOCK
    hpb = COL_BLOCK // LRU_BLOCK
    tm = TILE
    vec = pl.BlockSpec((1, COL_BLOCK), lambda c, bi, si: (0, c))
    return pl.pallas_call(
        _lru_kernel,
        grid=(ncb, b, s // tm),
        in_specs=[
            pl.BlockSpec((1, tm, d), lambda c, bi, si: (bi, si, 0)),
            pl.BlockSpec((d, COL_BLOCK), lambda c, bi, si: (0, c)),
            pl.BlockSpec((d, COL_BLOCK), lambda c, bi, si: (0, ncb + c)),
            pl.BlockSpec((CONV_WIDTH, COL_BLOCK), lambda c, bi, si: (0, c)),
            vec,
            pl.BlockSpec((hpb, LRU_BLOCK, LRU_BLOCK), lambda c, bi, si: (c, 0, 0)),
            vec,
            pl.BlockSpec((hpb, LRU_BLOCK, LRU_BLOCK), lambda c, bi, si: (c, 0, 0)),
            vec,
            vec,
        ],
        out_specs=pl.BlockSpec((1, tm, COL_BLOCK), lambda c, bi, si: (bi, si, c)),
        out_shape=jax.ShapeDtypeStruct((b, s, e), BF16),
        scratch_shapes=[pltpu.VMEM((CONV_HIST, COL_BLOCK), F32),
                        pltpu.VMEM((CONV_HIST + tm, COL_BLOCK), F32),
                        pltpu.VMEM((tm, COL_BLOCK), F32),
                        pltpu.VMEM((tm, COL_BLOCK), F32),
                        pltpu.VMEM((tm, COL_BLOCK), F32),
                        pltpu.VMEM((SUBLANES, COL_BLOCK), F32)],
        compiler_params=pltpu.CompilerParams(
            dimension_semantics=("arbitrary", "arbitrary", "arbitrary"),
            vmem_limit_bytes=VMEM_LIMIT),
        name="lru_mixer",
    )(h3, w_in, w_in, conv_w, conv_b, wa, ba, wx, bx, lam)


def _out_kernel(y_ref, x_ref, p_ref, wo_ref, wg_ref, wp_ref, gpost_ref, gin_ref, gout_ref, gnext_ref,
                xo_ref, hn_ref):
    o = jnp.dot(y_ref[...], wo_ref[...], preferred_element_type=F32)
    x1 = x_ref[...] + _rms(o, gpost_ref[...])
    xn = _rms(x1, gin_ref[...]).astype(BF16)
    gate = jax.nn.sigmoid(jnp.dot(xn, wg_ref[...], preferred_element_type=F32))
    emb = jnp.dot(p_ref[...], wp_ref[...], preferred_element_type=F32)
    x2 = x1 + _rms(emb * gate, gout_ref[...])
    xo_ref[...] = x2
    hn_ref[...] = _rms(x2, gnext_ref[...]).astype(BF16)


def _out_layer(y2, x2, p2, w_out, w_gate, w_ple, g_post, g_in, g_out, g_next):
    t, e = y2.shape
    d = x2.shape[1]
    k = p2.shape[1]
    tm = OUT_ROWS
    resident = pl.Buffered(1)
    vec = pl.BlockSpec((1, d), lambda i: (0, 0))
    return pl.pallas_call(
        _out_kernel,
        grid=(t // tm,),
        in_specs=[
            pl.BlockSpec((tm, e), lambda i: (i, 0)),
            pl.BlockSpec((tm, d), lambda i: (i, 0)),
            pl.BlockSpec((tm, k), lambda i: (i, 0)),
            pl.BlockSpec((e, d), lambda i: (0, 0), pipeline_mode=resident),
            pl.BlockSpec((d, d), lambda i: (0, 0), pipeline_mode=resident),
            pl.BlockSpec((k, d), lambda i: (0, 0), pipeline_mode=resident),
            vec, vec, vec, vec,
        ],
        out_specs=[pl.BlockSpec((tm, d), lambda i: (i, 0)),
                   pl.BlockSpec((tm, d), lambda i: (i, 0))],
        out_shape=[jax.ShapeDtypeStruct((t, d), F32),
                   jax.ShapeDtypeStruct((t, d), BF16)],
        compiler_params=pltpu.CompilerParams(
            dimension_semantics=("arbitrary",),
            vmem_limit_bytes=VMEM_LIMIT),
        name="out_ple",
    )(y2, x2, p2, w_out, w_gate, w_ple, g_post, g_in, g_out, g_next)


def kernel(x, p, w_in, w_out, g_pre, g_post, pool_w, pool_b, pool_scale, conv_w, conv_b,
           lru_wa, lru_ba, lru_wx, lru_bx, lru_L, w_ple, w_ple_gate, g_ple_in, g_ple_out):
    b, s, d = x.shape
    depth = w_in.shape[0]
    e = w_out.shape[1]
    t = b * s
    row = lambda v: v.reshape(1, -1)

    x2 = _to_tile_order(x).reshape(t, d)
    pt = _to_tile_order(p.astype(BF16).reshape(depth * b, s, -1)).reshape(depth, t, -1)
    h2 = _prenorm(x2, row(g_pre[0]))
    for i in range(depth):
        j = i // 2
        h3 = h2.reshape(b, s, d)
        w_in_i = w_in[i].astype(BF16)
        if i % 2 == 0:
            y3 = _pool_layer(h3, w_in_i, pool_w[j].astype(BF16), row(pool_b[j]), row(pool_scale[j]))
        else:
            y3 = _lru_layer(h3, w_in_i, conv_w[j], row(conv_b[j]),
                            lru_wa[j].astype(BF16), row(lru_ba[j]),
                            lru_wx[j].astype(BF16), row(lru_bx[j]), row(lru_L[j]))
        g_next = g_pre[i + 1] if i + 1 < depth else g_pre[i]
        x2, h2 = _out_layer(y3.reshape(t, e), x2, pt[i],
                            w_out[i].astype(BF16), w_ple_gate[i].astype(BF16), w_ple[i].astype(BF16),
                            row(g_post[i]), row(g_ple_in[i]), row(g_ple_out[i]), row(g_next))
    return _from_tile_order(x2.reshape(b, s, d))
```
